```python
import math
import jax, jax.numpy as jnp
from jax import lax
import numpy as np

D_MODEL = 1024
BATCH = 8
SEQ = 8192
DEPTH = 1

HEAD_DIM = 64
N_ATTN_HEADS = 8
ATTN_WIDTH = N_ATTN_HEADS * HEAD_DIM
N_SSM_HEADS = 24
SSM_WIDTH = N_SSM_HEADS * HEAD_DIM
MIX_WIDTH = ATTN_WIDTH + SSM_WIDTH
SSM_GROUPS = 2
SSM_STATE = 128
CONV_WIDTH = 4
CONV_DIM = SSM_WIDTH + 2 * SSM_GROUPS * SSM_STATE
SSD_CHUNK = 128
DILATED_PATTERNS = ((128, 1), (512, 4), (2048, 16))
IN_PROJ = 3 * ATTN_WIDTH + SSM_WIDTH + CONV_DIM + N_SSM_HEADS
D_FF = -(-8 * D_MODEL // (3 * 256)) * 256
PLE_DIM = 256
EPS = 1e-6

kernel_name = "hymba_ssd_dilated_swa_block"


def rms_norm(x, gain):
    xf = x.astype(jnp.float32)
    y = xf * lax.rsqrt(jnp.mean(xf * xf, axis=-1, keepdims=True) + EPS)
    return (y * gain.astype(jnp.float32)).astype(x.dtype)


def dilated_window_attention(q, k, v, window, dilation):
    b, s, h, dh = q.shape
    n = window // dilation
    span = n * dilation
    s_pad = -(-s // span) * span
    nb = s_pad // span

    def blocks(t):
        t = jnp.pad(t, ((0, 0), (0, s_pad - s), (0, 0), (0, 0)))
        return t.reshape(b, nb, n, dilation, h, dh)

    def with_prev(t):
        prev = jnp.pad(t[:, :-1], ((0, 0), (1, 0), (0, 0), (0, 0), (0, 0), (0, 0)))
        return jnp.concatenate([prev, t], axis=2)

    qb = blocks(q)
    kc = with_prev(blocks(k))
    vc = with_prev(blocks(v))
    scores = jnp.einsum('bnidhe,bnjdhe->bndhij', qb, kc,
                        preferred_element_type=jnp.float32)
    qi = jnp.arange(n)[:, None]
    kj = jnp.arange(2 * n)[None, :]
    dist = qi + n - kj
    band = (dist >= 0) & (dist <= n)
    first = (jnp.arange(nb) == 0)[:, None, None] & (kj < n)[None]
    mask = band[None] & ~first
    scores = jnp.where(mask[:, None, None], scores, -jnp.inf)
    m = jnp.max(scores, axis=-1, keepdims=True)
    e = jnp.exp(scores - m)
    den = jnp.sum(e, axis=-1, keepdims=True)
    out = jnp.einsum('bndhij,bnjdhe->bndhie', e, vc.astype(jnp.float32)) / den
    lse = (m + jnp.log(den))[..., 0]
    out = out.transpose(0, 1, 4, 2, 3, 5).reshape(b, s_pad, h, dh)[:, :s]
    lse = lse.transpose(0, 1, 4, 2, 3).reshape(b, s_pad, h)[:, :s]
    return out, lse


def mixture_of_dilations(q, k, v):
    outs, lses = [], []
    for window, dilation in DILATED_PATTERNS:
        o, l = dilated_window_attention(q, k, v, window, dilation)
        outs.append(o)
        lses.append(l)
    wts = jax.nn.softmax(jnp.stack(lses, axis=0), axis=0)
    return jnp.einsum('pbsh,pbshe->bshe', wts, jnp.stack(outs, axis=0))


def causal_depthwise_conv(u, w, bias):
    c = u.shape[-1]
    out = lax.conv_general_dilated(u, w[:, None, :].astype(u.dtype), window_strides=(1,),
                                   padding=[(w.shape[0] - 1, 0)],
                                   dimension_numbers=('NWC', 'WIO', 'NWC'),
                                   feature_group_count=c)
    return out + bias.astype(u.dtype)


def ssd_chunked(x, dt, a, bm, cm):
    b, s, h, pdim = x.shape
    g, n = bm.shape[2], bm.shape[3]
    hg = h // g
    q = SSD_CHUNK
    s_pad = -(-s // q) * q
    nc = s_pad // q

    def pad(t):
        return jnp.pad(t, [(0, 0), (0, s_pad - s)] + [(0, 0)] * (t.ndim - 2))

    xc = pad(x * dt[..., None]).reshape(b, nc, q, g, hg, pdim)
    da = pad(dt * a).reshape(b, nc, q, g, hg)
    bc = pad(bm).reshape(b, nc, q, g, n)
    cc = pad(cm).reshape(b, nc, q, g, n)
    acs = jnp.cumsum(da, axis=2)

    seg = acs[:, :, :, None] - acs[:, :, None, :]
    causal = jnp.tril(jnp.ones((q, q), dtype=bool))
    decay = jnp.exp(jnp.where(causal[:, :, None, None], seg, -jnp.inf))
    cb = jnp.einsum('bclgn,bcsgn->bclsg', cc, bc)
    y_diag = jnp.einsum('bclsgh,bcsghp->bclghp', cb[..., None] * decay, xc)

    decay_to_end = jnp.exp(acs[:, :, -1:] - acs)
    states = jnp.einsum('bcsgn,bcsgh,bcsghp->bcghpn', bc, decay_to_end, xc)
    chunk_decay = jnp.exp(acs[:, :, -1])

    def step(carry, inp):
        st, dec = inp
        return carry * dec[..., None, None] + st, carry

    init = jnp.zeros((b, g, hg, pdim, n), jnp.float32)
    _, prev_states = lax.scan(step, init, (jnp.moveaxis(states, 1, 0),
                                           jnp.moveaxis(chunk_decay, 1, 0)))
    prev_states = jnp.moveaxis(prev_states, 0, 1)
    y_off = jnp.einsum('bclgn,bcghpn,bclgh->bclghp', cc, prev_states, jnp.exp(acs))
    return (y_diag + y_off).reshape(b, s_pad, h, pdim)[:, :s]


def gated_group_rms_norm(y, z, gain):
    b, s, w = y.shape
    u = (y.astype(jnp.float32) * jax.nn.silu(z.astype(jnp.float32))).reshape(b, s, SSM_GROUPS, w // SSM_GROUPS)
    u = u * lax.rsqrt(jnp.mean(u * u, axis=-1, keepdims=True) + EPS)
    return u.reshape(b, s, w) * gain.astype(jnp.float32)


def setup_inputs(seed: int = 0) -> dict:
    key = jax.random.key(seed)
    ks = jax.random.split(key, 24)
    L = DEPTH
    f32 = jnp.float32

    def dense(k, shape, fan_in):
        return jax.random.normal(k, shape, f32) * fan_in ** -0.5

    def gain(k, shape):
        return 1.0 + 0.02 * jax.random.normal(k, shape, f32)

    dt0 = jnp.exp(jax.random.uniform(ks[10], (L, N_SSM_HEADS), f32,
                                     minval=math.log(1e-3), maxval=math.log(1e-1)))
    return {
        "x": jax.random.normal(ks[0], (BATCH, SEQ, D_MODEL), f32),
        "p": jax.random.normal(ks[1], (DEPTH, BATCH, SEQ, PLE_DIM), f32),
        "mix_norm": gain(ks[2], (L, D_MODEL)),
        "w_in": dense(ks[3], (L, D_MODEL, IN_PROJ), D_MODEL),
        "q_norm": gain(ks[4], (L, HEAD_DIM)),
        "k_norm": gain(ks[5], (L, HEAD_DIM)),
        "conv_w": dense(ks[6], (L, CONV_WIDTH, CONV_DIM), CONV_WIDTH),
        "conv_b": 0.02 * jax.random.normal(ks[7], (L, CONV_DIM), f32),
        "dt_bias": dt0 + jnp.log(-jnp.expm1(-dt0)),
        "a_log": jnp.log(jax.random.uniform(ks[8], (L, N_SSM_HEADS), f32, minval=1.0, maxval=16.0)),
        "d_skip": gain(ks[9], (L, N_SSM_HEADS)),
        "ssm_norm": gain(ks[11], (L, SSM_WIDTH)),
        "w_out": dense(ks[12], (L, MIX_WIDTH, D_MODEL), MIX_WIDTH),
        "ffn_norm": gain(ks[13], (L, D_MODEL)),
        "w_ffn_gate": dense(ks[14], (L, D_MODEL, D_FF), D_MODEL),
        "w_ffn_up": dense(ks[15], (L, D_MODEL, D_FF), D_MODEL),
        "w_ffn_down": dense(ks[16], (L, D_FF, D_MODEL), D_FF),
        "ple_gate_norm": gain(ks[17], (L, D_MODEL)),
        "w_ple_gate": dense(ks[18], (L, D_MODEL, D_MODEL), D_MODEL),
        "b_ple_gate": 0.02 * jax.random.normal(ks[19], (L, D_MODEL), f32),
        "w_ple": dense(ks[20], (L, PLE_DIM, D_MODEL), PLE_DIM),
        "ple_norm": gain(ks[21], (L, D_MODEL)),
    }


def reference(x, p, mix_norm, w_in, q_norm, k_norm, conv_w, conv_b, dt_bias, a_log, d_skip,
              ssm_norm, w_out, ffn_norm, w_ffn_gate, w_ffn_up, w_ffn_down, ple_gate_norm,
              w_ple_gate, b_ple_gate, w_ple, ple_norm):
    b, s, _ = x.shape
    splits = [ATTN_WIDTH, 2 * ATTN_WIDTH, 3 * ATTN_WIDTH,
              3 * ATTN_WIDTH + SSM_WIDTH, 3 * ATTN_WIDTH + SSM_WIDTH + CONV_DIM]
    for i in range(DEPTH):
        h = rms_norm(x, mix_norm[i])
        proj = h @ w_in[i]
        q, k, v, z, xbc, dt_raw = jnp.split(proj, splits, axis=-1)

        q = rms_norm(q.reshape(b, s, N_ATTN_HEADS, HEAD_DIM), q_norm[i]) * (HEAD_DIM ** -0.5)
        k = rms_norm(k.reshape(b, s, N_ATTN_HEADS, HEAD_DIM), k_norm[i])
        v = v.reshape(b, s, N_ATTN_HEADS, HEAD_DIM)
        attn = mixture_of_dilations(q, k, v).reshape(b, s, ATTN_WIDTH)

        xbc = jax.nn.silu(causal_depthwise_conv(xbc, conv_w[i], conv_b[i]))
        xs, bm, cm = jnp.split(xbc, [SSM_WIDTH, SSM_WIDTH + SSM_GROUPS * SSM_STATE], axis=-1)
        xs = xs.reshape(b, s, N_SSM_HEADS, HEAD_DIM).astype(jnp.float32)
        dt = jax.nn.softplus(dt_raw.astype(jnp.float32) + dt_bias[i].astype(jnp.float32))
        a = -jnp.exp(a_log[i].astype(jnp.float32))
        y = ssd_chunked(xs, dt, a,
                        bm.reshape(b, s, SSM_GROUPS, SSM_STATE).astype(jnp.float32),
                        cm.reshape(b, s, SSM_GROUPS, SSM_STATE).astype(jnp.float32))
        y = y + d_skip[i].astype(jnp.float32)[:, None] * xs
        y = gated_group_rms_norm(y.reshape(b, s, SSM_WIDTH), z, ssm_norm[i])

        mixed = jnp.concatenate([attn.astype(x.dtype), y.astype(x.dtype)], axis=-1)
        x = x + mixed @ w_out[i]

        h = rms_norm(x, ffn_norm[i])
        x = x + (jax.nn.silu(h @ w_ffn_gate[i]) * (h @ w_ffn_up[i])) @ w_ffn_down[i]

        gate = jax.nn.sigmoid(rms_norm(x, ple_gate_norm[i]) @ w_ple_gate[i] + b_ple_gate[i])
        e = rms_norm(p[i] @ w_ple[i], ple_norm[i])
        x = x + gate * e
    return x
```

```python
import functools
import math

import jax
import jax.numpy as jnp
import numpy as np
from jax import lax
from jax.experimental import pallas as pl
from jax.experimental.pallas import tpu as pltpu

HEAD_DIM = 64
N_ATTN_HEADS = 8
ATTN_WIDTH = N_ATTN_HEADS * HEAD_DIM
N_SSM_HEADS = 24
SSM_WIDTH = N_SSM_HEADS * HEAD_DIM
SSM_GROUPS = 2
SSM_STATE = 128
GROUP_WIDTH = SSM_WIDTH // SSM_GROUPS
HEADS_PER_GROUP = N_SSM_HEADS // SSM_GROUPS
CONV_WIDTH = 4
BC_WIDTH = SSM_GROUPS * SSM_STATE
CONV_DIM = SSM_WIDTH + 2 * BC_WIDTH
DILATED_PATTERNS = ((128, 1), (512, 4), (2048, 16))
EPS = 1e-6

LANES = 128
SPAN = 2048
RES = 16
ROWS = SPAN // RES
WIN = 128
N_SLABS = ATTN_WIDTH // LANES
CHUNK = 128
NEG = -1e30
LOG2E = math.log2(math.e)
VMEM_LIMIT = 56 * 1024 * 1024


def _silu(v):
    return v / (1.0 + jnp.exp(-v))


def _split3(v):
    hi = v.astype(jnp.bfloat16)
    r1 = v - hi.astype(jnp.float32)
    mid = r1.astype(jnp.bfloat16)
    lo = (r1 - mid.astype(jnp.float32)).astype(jnp.bfloat16)
    return hi, mid, lo


def _dot(a, b):
    return jnp.dot(a, b, preferred_element_type=jnp.float32)


def _dot_nt(a, b):
    return lax.dot_general(a, b, (((1,), (1,)), ((), ())), preferred_element_type=jnp.float32)


def _dot_tn(a, b):
    return lax.dot_general(a, b, (((0,), (0,)), ((), ())), preferred_element_type=jnp.float32)


def _in_proj_body(x_ref, wqkv_ref, wz_ref, wxbc_ref, wdt_ref, mixg_ref, qg_ref, kg_ref, hsum_ref,
                  convw_ref, convb_ref, dtb_ref,
                  q_out, k_out, v_out, gz_out, xs_out, b_out, c_out, dt_out,
                  ext_scr, perm_scr, *, tm):
    s = pl.program_id(1)
    x = x_ref[...]
    ms = jnp.mean(x * x, axis=-1, keepdims=True)
    h = (x * lax.rsqrt(ms + EPS) * mixg_ref[...]).astype(jnp.bfloat16)

    def head_norm(t, gain):
        t2 = t * t
        hi = t2.astype(jnp.bfloat16)
        lo = (t2 - hi.astype(jnp.float32)).astype(jnp.bfloat16)
        ss = _dot(hi, hsum_ref[...]) + _dot(lo, hsum_ref[...])
        return t * lax.rsqrt(ss * (1.0 / HEAD_DIM) + EPS) * gain

    for a, out in enumerate((q_out, k_out, v_out)):
        t = _dot(h, wqkv_ref[:, a * ATTN_WIDTH:(a + 1) * ATTN_WIDTH])
        if a == 0:
            t = head_norm(t, qg_ref[...])
        elif a == 1:
            t = head_norm(t, kg_ref[...])
        for sl in range(N_SLABS):
            perm_scr[a * N_SLABS + sl] = t[:, sl * LANES:(sl + 1) * LANES]
        for sl in range(N_SLABS):
            for r in range(RES):
                rows = perm_scr[a * N_SLABS + sl, pl.ds(r, tm // RES, stride=RES), :]
                out[sl, r] = rows.astype(jnp.bfloat16)

    gz_out[...] = _silu(_dot(h, wz_ref[...])).astype(jnp.bfloat16)

    @pl.when(s == 0)
    def _():
        ext_scr[0:8, :] = jnp.zeros((8, CONV_DIM), jnp.float32)

    ext_scr[8:8 + tm, :] = _dot(h, wxbc_ref[...])
    acc = convb_ref[...] + convw_ref[CONV_WIDTH - 1:CONV_WIDTH, :] * ext_scr[8:8 + tm, :]
    for j in range(1, CONV_WIDTH):
        acc = acc + convw_ref[CONV_WIDTH - 1 - j:CONV_WIDTH - j, :] * ext_scr[8 - j:8 - j + tm, :]
    act = _silu(acc).astype(jnp.bfloat16)
    xs_out[...] = act[:, :SSM_WIDTH]
    b_out[...] = act[:, SSM_WIDTH:SSM_WIDTH + BC_WIDTH]
    c_out[...] = act[:, SSM_WIDTH + BC_WIDTH:]
    ext_scr[0:8, :] = ext_scr[tm:tm + 8, :]

    dt_out[...] = jax.nn.softplus(_dot(h, wdt_ref[...]) + dtb_ref[...])


def _in_proj(x, wqkv, wz, wxbc, wdt, mixg, qg, kg, hsum, convw, convb, dtb, *, tm):
    b, s, d = x.shape
    ns = s // SPAN
    tps = SPAN // tm
    grid = (b, s // tm)
    const = lambda shape: pl.BlockSpec(shape, lambda i, j: (0,) * len(shape), pipeline_mode=pl.Buffered(1))
    tok = lambda w: pl.BlockSpec((None, tm, w), lambda i, j: (i, j, 0))
    qkv_spec = pl.BlockSpec((None, None, N_SLABS, RES, tm // RES, LANES),
                            lambda i, j: (i, j // tps, 0, 0, j % tps, 0))
    qkv_shape = jax.ShapeDtypeStruct((b, ns, N_SLABS, RES, ROWS, LANES), jnp.bfloat16)
    return pl.pallas_call(
        functools.partial(_in_proj_body, tm=tm),
        grid=grid,
        in_specs=[tok(d), const(wqkv.shape), const(wz.shape), const(wxbc.shape), const(wdt.shape),
                  const(mixg.shape), const(qg.shape), const(kg.shape), const(hsum.shape),
                  const(convw.shape), const(convb.shape), const(dtb.shape)],
        out_specs=[qkv_spec, qkv_spec, qkv_spec, tok(SSM_WIDTH), tok(SSM_WIDTH), tok(BC_WIDTH), tok(BC_WIDTH),
                   tok(LANES)],
        out_shape=[qkv_shape, qkv_shape, qkv_shape,
                   jax.ShapeDtypeStruct((b, s, SSM_WIDTH), jnp.bfloat16),
                   jax.ShapeDtypeStruct((b, s, SSM_WIDTH), jnp.bfloat16),
                   jax.ShapeDtypeStruct((b, s, BC_WIDTH), jnp.bfloat16),
                   jax.ShapeDtypeStruct((b, s, BC_WIDTH), jnp.bfloat16),
                   jax.ShapeDtypeStruct((b, s, LANES), jnp.float32)],
        scratch_shapes=[pltpu.VMEM((tm + 8, CONV_DIM), jnp.float32),
                        pltpu.VMEM((3 * N_SLABS, tm, LANES), jnp.float32)],
        compiler_params=pltpu.CompilerParams(dimension_semantics=("arbitrary", "arbitrary"),
                                             vmem_limit_bytes=VMEM_LIMIT),
        name="in_proj",
    )(x, wqkv, wz, wxbc, wdt, mixg, qg, kg, hsum, convw, convb, dtb)


def _attention_masks():
    rho = np.arange(WIN)
    col = np.arange(2 * WIN)
    uq0 = 16 * (rho % 8) + rho // 8
    half0 = (col % 16) // 8
    uk0 = 16 * (col % 8) + col // 16 + (half0 - 1) * WIN
    uq1 = 4 * (rho % 32) + rho // 32
    half1 = (col % 64) // 32
    uk1 = 4 * (col % 32) + col // 64 + (half1 - 1) * WIN
    uq2 = rho
    half2 = col // WIN
    uk2 = col % WIN + (half2 - 1) * WIN
    out = np.zeros((3, 2, WIN, 2 * WIN), np.float32)
    for p, (uq, uk, half) in enumerate(((uq0, uk0, half0), (uq1, uk1, half1), (uq2, uk2, half2))):
        dist = uq[:, None] - uk[None, :]
        vis = (dist >= 0) & (dist <= WIN)
        out[p, 0] = np.where(vis, 0.0, NEG)
        out[p, 1] = np.where(vis & (half[None, :] == 1), 0.0, NEG)
    return out.reshape(6, WIN, 2 * WIN)


def _attn_unit(q, k, v, mask):
    lo_q = lax.broadcasted_iota(jnp.int32, (1, LANES), 1) < HEAD_DIM
    zero = jnp.zeros((), jnp.bfloat16)
    qq = jnp.concatenate([jnp.where(lo_q, q, zero), jnp.where(lo_q, zero, q)], axis=0)
    sc = _dot_nt(qq, k)
    sa = sc[:WIN] + mask
    sb = sc[WIN:] + mask
    ma = jnp.max(sa, axis=-1, keepdims=True)
    mb = jnp.max(sb, axis=-1, keepdims=True)
    pa = jnp.exp2(sa - ma)
    pb = jnp.exp2(sb - mb)
    la = jnp.sum(pa, axis=-1, keepdims=True)
    lb = jnp.sum(pb, axis=-1, keepdims=True)
    va = jnp.where(lo_q, v, zero)
    vb = jnp.where(lo_q, zero, v)
    o = _dot(pa.astype(jnp.bfloat16), va) + _dot(pb.astype(jnp.bfloat16), vb)
    m = jnp.where(lo_q, ma, mb)
    l = jnp.where(lo_q, la, lb)
    return o, m, l


def _attn_merge(o, m, l, ao, am, al):
    mx = jnp.maximum(m, am)
    wa = jnp.exp2(am - mx)
    wb = jnp.exp2(m - mx)
    return ao * wa + o * wb, mx, al * wa + l * wb


def _attention_body(q_ref, k_ref, kp_ref, v_ref, vp_ref, mask_ref, o_ref,
                    kc, vc, qf, kcf, vcf, acc_o, acc_m, acc_l, nat):
    first_span = pl.program_id(1) == 0
    tail = ROWS // 4
    kc[:, 0:tail, :] = kp_ref[:, ROWS - tail:, :]
    vc[:, 0:tail, :] = vp_ref[:, ROWS - tail:, :]
    kc[:, tail:, :] = k_ref[...]
    vc[:, tail:, :] = v_ref[...]
    qf[...] = q_ref[...].astype(jnp.float32)
    kcf[...] = kc[...].astype(jnp.float32)
    vcf[...] = vc[...].astype(jnp.float32)

    def p3(r, carry):
        q = q_ref[r]
        k = jnp.concatenate([kp_ref[r], k_ref[r]], axis=0)
        v = jnp.concatenate([vp_ref[r], v_ref[r]], axis=0)
        o, m, l = _attn_unit(q, k, v, mask_ref[4 + first_span.astype(jnp.int32)])
        acc_o[r] = o
        acc_m[r] = m
        acc_l[r] = l
        return carry

    lax.fori_loop(0, RES, p3, 0)

    def p2(t, carry):
        b2 = t // 4
        r4 = t % 4
        off = pl.multiple_of(b2 * tail, tail)
        q = jnp.concatenate([q_ref[r4 + 4 * j, pl.ds(off, tail), :] for j in range(4)], axis=0)
        k = jnp.concatenate([kc[r4 + 4 * j, pl.ds(off, 2 * tail), :] for j in range(4)], axis=0)
        v = jnp.concatenate([vc[r4 + 4 * j, pl.ds(off, 2 * tail), :] for j in range(4)], axis=0)
        first = jnp.logical_and(first_span, b2 == 0).astype(jnp.int32)
        o, m, l = _attn_unit(q, k, v, mask_ref[2 + first])
        for j in range(4):
            rows = slice(j * tail, (j + 1) * tail)
            no, nm, nl = _attn_merge(o[rows], m[rows], l[rows],
                                     acc_o[r4 + 4 * j, pl.ds(off, tail), :],
                                     acc_m[r4 + 4 * j, pl.ds(off, tail), :],
                                     acc_l[r4 + 4 * j, pl.ds(off, tail), :])
            acc_o[r4 + 4 * j, pl.ds(off, tail), :] = no
            acc_m[r4 + 4 * j, pl.ds(off, tail), :] = nm
            acc_l[r4 + 4 * j, pl.ds(off, tail), :] = nl
        return carry

    lax.fori_loop(0, 16, p2, 0)

    def p1(b, carry):
        off = pl.multiple_of(b * 8, 8)
        koff = pl.multiple_of(b * 8 + (tail - 8), 8)
        q = jnp.concatenate([qf[r, pl.ds(off, 8), :] for r in range(RES)], axis=0).astype(jnp.bfloat16)
        k = jnp.concatenate([kcf[r, pl.ds(koff, 16), :] for r in range(RES)], axis=0).astype(jnp.bfloat16)
        v = jnp.concatenate([vcf[r, pl.ds(koff, 16), :] for r in range(RES)], axis=0).astype(jnp.bfloat16)
        first = jnp.logical_and(first_span, b == 0).astype(jnp.int32)
        o, m, l = _attn_unit(q, k, v, mask_ref[first])
        for r in range(RES):
            rows = slice(r * 8, (r + 1) * 8)
            no, _, nl = _attn_merge(o[rows], m[rows], l[rows],
                                    acc_o[r, pl.ds(off, 8), :], acc_m[r, pl.ds(off, 8), :],
                                    acc_l[r, pl.ds(off, 8), :])
            acc_o[r, pl.ds(off, 8), :] = no / nl
        return carry

    lax.fori_loop(0, SPAN // WIN, p1, 0)

    for r in range(RES):
        nat[pl.ds(r, ROWS, stride=RES), :] = acc_o[r]
    o_ref[...] = nat[...].astype(jnp.bfloat16)


def _attention(qr, kr, vr, masks):
    b, ns = qr.shape[0], qr.shape[1]
    cur = pl.BlockSpec((None, None, None, RES, ROWS, LANES), lambda i, j, c: (i, j, c, 0, 0, 0))
    prev = pl.BlockSpec((None, None, None, RES, ROWS, LANES), lambda i, j, c: (i, jnp.maximum(j - 1, 0), c, 0, 0, 0))
    f32 = jnp.float32
    stage_rows = ROWS + ROWS // 4
    return pl.pallas_call(
        _attention_body,
        grid=(b, ns, N_SLABS),
        in_specs=[cur, cur, prev, cur, prev, pl.BlockSpec(masks.shape, lambda i, j, c: (0, 0, 0))],
        out_specs=pl.BlockSpec((None, SPAN, LANES), lambda i, j, c: (i, j, c)),
        out_shape=jax.ShapeDtypeStruct((b, ns * SPAN, ATTN_WIDTH), jnp.bfloat16),
        scratch_shapes=[pltpu.VMEM((RES, stage_rows, LANES), jnp.bfloat16),
                        pltpu.VMEM((RES, stage_rows, LANES), jnp.bfloat16),
                        pltpu.VMEM((RES, ROWS, LANES), f32),
                        pltpu.VMEM((RES, stage_rows, LANES), f32),
                        pltpu.VMEM((RES, stage_rows, LANES), f32),
                        pltpu.VMEM((RES, ROWS, LANES), f32),
                        pltpu.VMEM((RES, ROWS, LANES), f32),
                        pltpu.VMEM((RES, ROWS, LANES), f32),
                        pltpu.VMEM((SPAN, LANES), f32)],
        compiler_params=pltpu.CompilerParams(dimension_semantics=("arbitrary", "arbitrary", "arbitrary"),
                                             vmem_limit_bytes=VMEM_LIMIT),
        name="attention",
    )(qr, kr, kr, vr, vr, masks)


def _ssd_body(xs_ref, b_ref, c_ref, dt_ref, gz_ref, a_ref, dskip_ref, gain_ref, ltri_ref, expand_ref,
              y_ref, state, *, n_chunks):
    @pl.when(pl.program_id(1) == 0)
    def _():
        state[...] = jnp.zeros_like(state)

    lane_lo = lax.broadcasted_iota(jnp.int32, (1, LANES), 1) < HEAD_DIM
    causal = (lax.broadcasted_iota(jnp.int32, (CHUNK, CHUNK), 0)
              >= lax.broadcasted_iota(jnp.int32, (CHUNK, CHUNK), 1))
    zero = jnp.zeros((), jnp.bfloat16)

    for c in range(n_chunks):
        rows = slice(c * CHUNK, (c + 1) * CHUNK)
        xs = xs_ref[rows, :]
        bm = b_ref[rows, :]
        cm = c_ref[rows, :]
        dt = dt_ref[rows, :]
        da = dt * a_ref[...]
        d_hi, d_mid, d_lo = _split3(da)
        ltri = ltri_ref[...]
        acs = _dot(ltri, d_hi) + _dot(ltri, d_mid) + _dot(ltri, d_lo)
        total = acs[CHUNK - 1:CHUNK, :]
        shift_t = (acs - jnp.log(dt)).T

        w_end = dt * jnp.exp(total - acs)
        w_hi = w_end.astype(jnp.bfloat16)
        e_acs = jnp.exp(acs)
        e_hi = e_acs.astype(jnp.bfloat16)
        w_exp = _dot(w_hi, expand_ref[...])
        e_exp = _dot(e_hi, expand_ref[...])
        dec = jnp.exp(total)
        dec8 = jnp.broadcast_to(dec, (8, LANES))
        dc_hi = dec8.astype(jnp.bfloat16)
        dc_lo = (dec8 - dc_hi.astype(jnp.float32)).astype(jnp.bfloat16)
        dec_exp = (_dot(dc_hi, expand_ref[...]) + _dot(dc_lo, expand_ref[...]))[0:1, :]

        xw = (xs.astype(jnp.float32) * w_exp).astype(jnp.bfloat16)

        for g in range(SSM_GROUPS):
            bg = bm[:, g * SSM_STATE:(g + 1) * SSM_STATE]
            cg = cm[:, g * SSM_STATE:(g + 1) * SSM_STATE]
            gmat = _dot_nt(cg, bg)
            glanes = slice(g * GROUP_WIDTH, (g + 1) * GROUP_WIDTH)
            st = state[g]
            y_off = _dot(cg, st.astype(jnp.bfloat16)) * e_exp[:, glanes]
            state[g] = st * dec_exp[:, glanes] + _dot_tn(bg, xw[:, glanes])
            y_parts = []
            for pair in range(HEADS_PER_GROUP // 2):
                h0 = g * HEADS_PER_GROUP + 2 * pair
                ms = []
                for hh in (h0, h0 + 1):
                    e = acs[:, hh:hh + 1] - shift_t[hh:hh + 1, :]
                    e = jnp.where(causal, e, NEG)
                    ms.append((gmat * jnp.exp(e)).astype(jnp.bfloat16))
                slab = xs[:, h0 * HEAD_DIM:(h0 + 2) * HEAD_DIM]
                rhs = jnp.concatenate([jnp.where(lane_lo, slab, zero), jnp.where(lane_lo, zero, slab)], axis=0)
                y_parts.append(_dot(jnp.concatenate(ms, axis=1), rhs))
            y = jnp.concatenate(y_parts, axis=1) + y_off
            xg = xs[:, glanes].astype(jnp.float32)
            y = y + dskip_ref[:, glanes] * xg
            u = y * gz_ref[rows, glanes].astype(jnp.float32)
            u = u * lax.rsqrt(jnp.mean(u * u, axis=-1, keepdims=True) + EPS)
            y_ref[rows, glanes] = (u * gain_ref[:, glanes]).astype(jnp.bfloat16)


def _ssd(xs, bm, cm, dt, gz, a_row, dskip, gain, ltri, expand, *, n_chunks):
    b, s, _ = xs.shape
    tt = n_chunks * CHUNK
    tok = lambda w: pl.BlockSpec((None, tt, w), lambda i, j: (i, j, 0))
    const = lambda arr: pl.BlockSpec(arr.shape, lambda i, j: (0,) * arr.ndim)
    return pl.pallas_call(
        functools.partial(_ssd_body, n_chunks=n_chunks),
        grid=(b, s // tt),
        in_specs=[tok(SSM_WIDTH), tok(BC_WIDTH), tok(BC_WIDTH), tok(LANES), tok(SSM_WIDTH),
                  const(a_row), const(dskip), const(gain), const(ltri), const(expand)],
        out_specs=tok(SSM_WIDTH),
        out_shape=jax.ShapeDtypeStruct((b, s, SSM_WIDTH), jnp.bfloat16),
        scratch_shapes=[pltpu.VMEM((SSM_GROUPS, SSM_STATE, GROUP_WIDTH), jnp.float32)],
        compiler_params=pltpu.CompilerParams(dimension_semantics=("arbitrary", "arbitrary"),
                                             vmem_limit_bytes=VMEM_LIMIT),
        name="ssd",
    )(xs, bm, cm, dt, gz, a_row, dskip, gain, ltri, expand)


def _rms(v, gain):
    return v * lax.rsqrt(jnp.mean(v * v, axis=-1, keepdims=True) + EPS) * gain


def _out_ffn_body(x_ref, attn_ref, y_ref, p_ref, wo_a_ref, wo_y_ref, fg_ref, wg_ref, wu_ref, wd_ref,
                  pg_ref, wpg_ref, bpg_ref, wple_ref, pn_ref, o_ref):
    x = x_ref[...] + _dot(attn_ref[...], wo_a_ref[...]) + _dot(y_ref[...], wo_y_ref[...])
    h = _rms(x, fg_ref[...]).astype(jnp.bfloat16)
    act = (_silu(_dot(h, wg_ref[...])) * _dot(h, wu_ref[...])).astype(jnp.bfloat16)
    x = x + _dot(act, wd_ref[...])
    hg = _rms(x, pg_ref[...]).astype(jnp.bfloat16)
    z = _dot(hg, wpg_ref[...]) + bpg_ref[...]
    gate = 1.0 / (1.0 + jnp.exp(-z))
    e = _rms(_dot(p_ref[...].astype(jnp.bfloat16), wple_ref[...]), pn_ref[...])
    o_ref[...] = x + gate * e


def _out_ffn(x, attn, y, p, wo_a, wo_y, fg, wg, wu, wd, pg, wpg, bpg, wple, pn, *, tm):
    b, s, d = x.shape
    tok = lambda w: pl.BlockSpec((None, tm, w), lambda i, j: (i, j, 0))
    const = lambda arr: pl.BlockSpec(arr.shape, lambda i, j: (0,) * arr.ndim, pipeline_mode=pl.Buffered(1))
    return pl.pallas_call(
        _out_ffn_body,
        grid=(b, s // tm),
        in_specs=[tok(d), tok(ATTN_WIDTH), tok(SSM_WIDTH), tok(p.shape[-1]),
                  const(wo_a), const(wo_y), const(fg), const(wg), const(wu), const(wd),
                  const(pg), const(wpg), const(bpg), const(wple), const(pn)],
        out_specs=tok(d),
        out_shape=jax.ShapeDtypeStruct((b, s, d), jnp.float32),
        compiler_params=pltpu.CompilerParams(dimension_semantics=("arbitrary", "arbitrary"),
                                             vmem_limit_bytes=VMEM_LIMIT),
        name="out_ffn",
    )(x, attn, y, p, wo_a, wo_y, fg, wg, wu, wd, pg, wpg, bpg, wple, pn)


def _layer(x, p, mix_norm, w_in, q_norm, k_norm, conv_w, conv_b, dt_bias, a_log, d_skip, ssm_norm, w_out,
           ffn_norm, w_ffn_gate, w_ffn_up, w_ffn_down, ple_gate_norm, w_ple_gate, b_ple_gate, w_ple, ple_norm):
    bf16 = jnp.bfloat16
    f32 = jnp.float32
    d = x.shape[-1]
    row = lambda v: v.reshape(1, -1).astype(f32)
    o_z = 3 * ATTN_WIDTH
    o_xbc = o_z + SSM_WIDTH
    o_dt = o_xbc + CONV_DIM
    wqkv = w_in[:, :o_z].astype(bf16)
    wz = w_in[:, o_z:o_xbc].astype(bf16)
    wxbc = w_in[:, o_xbc:o_dt].astype(bf16)
    wdt = jnp.pad(w_in[:, o_dt:], ((0, 0), (0, LANES - N_SSM_HEADS))).astype(bf16)
    dtb = jnp.pad(dt_bias.astype(f32), (0, LANES - N_SSM_HEADS)).reshape(1, LANES)
    qg = row(jnp.tile(q_norm, N_ATTN_HEADS)) * (HEAD_DIM ** -0.5 * LOG2E)
    kg = row(jnp.tile(k_norm, N_ATTN_HEADS))
    head_of = np.arange(ATTN_WIDTH) // HEAD_DIM
    hsum = jnp.asarray(head_of[:, None] == head_of[None, :], bf16)

    qr, kr, vr, gz, xs, bm, cm, dt = _in_proj(x, wqkv, wz, wxbc, wdt, row(mix_norm), qg, kg, hsum,
                                              conv_w.astype(f32), row(conv_b), dtb, tm=512)

    attn = _attention(qr, kr, vr, jnp.asarray(_attention_masks()))

    a_row = jnp.pad(-jnp.exp(a_log.astype(f32)), (0, LANES - N_SSM_HEADS)).reshape(1, LANES)
    tri = np.tril(np.ones((CHUNK, CHUNK), np.float32))
    ssm_head_of = np.arange(SSM_WIDTH) // HEAD_DIM
    expand = jnp.asarray(np.arange(LANES)[:, None] == ssm_head_of[None, :], bf16)
    y = _ssd(xs, bm, cm, dt, gz, a_row, row(jnp.repeat(d_skip, HEAD_DIM)), row(ssm_norm),
             jnp.asarray(tri, bf16), expand, n_chunks=4)

    return _out_ffn(x, attn, y, p,
                    w_out[:ATTN_WIDTH].astype(bf16), w_out[ATTN_WIDTH:].astype(bf16), row(ffn_norm),
                    w_ffn_gate.astype(bf16), w_ffn_up.astype(bf16), w_ffn_down.astype(bf16),
                    row(ple_gate_norm), w_ple_gate.astype(bf16), row(b_ple_gate), w_ple.astype(bf16),
                    row(ple_norm), tm=512)


def kernel(x, p, mix_norm, w_in, q_norm, k_norm, conv_w, conv_b, dt_bias, a_log, d_skip, ssm_norm, w_out, ffn_norm,
           w_ffn_gate, w_ffn_up, w_ffn_down, ple_gate_norm, w_ple_gate, b_ple_gate, w_ple, ple_norm):
    for i in range(p.shape[0]):
        x = _layer(x, p[i], mix_norm[i], w_in[i], q_norm[i], k_norm[i], conv_w[i], conv_b[i], dt_bias[i], a_log[i],
                   d_skip[i], ssm_norm[i], w_out[i], ffn_norm[i], w_ffn_gate[i], w_ffn_up[i], w_ffn_down[i],
                   ple_gate_norm[i], w_ple_gate[i], b_ple_gate[i], w_ple[i], ple_norm[i])
    return x
```

```python
import functools
import math

import jax
import jax.numpy as jnp
import numpy as np
from jax import lax
from jax.experimental import pallas as pl
from jax.experimental.pallas import tpu as pltpu

HEAD_DIM = 64
N_ATTN_HEADS = 8
ATTN_WIDTH = N_ATTN_HEADS * HEAD_DIM
N_SSM_HEADS = 24
SSM_WIDTH = N_SSM_HEADS * HEAD_DIM
SSM_GROUPS = 2
SSM_STATE = 128
GROUP_WIDTH = SSM_WIDTH // SSM_GROUPS
HEADS_PER_GROUP = N_SSM_HEADS // SSM_GROUPS
CONV_WIDTH = 4
BC_WIDTH = SSM_GROUPS * SSM_STATE
CONV_DIM = SSM_WIDTH + 2 * BC_WIDTH
DILATED_PATTERNS = ((128, 1), (512, 4), (2048, 16))
EPS = 1e-6

LANES = 128
SPAN = 2048
RES = 16
ROWS = SPAN // RES
WIN = 128
N_SLABS = ATTN_WIDTH // LANES
CHUNK = 128
NEG = -1e30
LOG2E = math.log2(math.e)
VMEM_LIMIT = 56 * 1024 * 1024


def _silu(v):
    return v / (1.0 + jnp.exp(-v))


def _split3(v):
    hi = v.astype(jnp.bfloat16)
    r1 = v - hi.astype(jnp.float32)
    mid = r1.astype(jnp.bfloat16)
    lo = (r1 - mid.astype(jnp.float32)).astype(jnp.bfloat16)
    return hi, mid, lo


def _dot(a, b):
    return jnp.dot(a, b, preferred_element_type=jnp.float32)


def _dot_nt(a, b):
    return lax.dot_general(a, b, (((1,), (1,)), ((), ())), preferred_element_type=jnp.float32)


def _dot_tn(a, b):
    return lax.dot_general(a, b, (((0,), (0,)), ((), ())), preferred_element_type=jnp.float32)


def _in_proj_body(x_ref, wqkv_ref, wz_ref, wxbc_ref, wdt_ref, mixg_ref, qg_ref, kg_ref, hsum_ref,
                  convw_ref, convb_ref, dtb_ref,
                  q_out, k_out, v_out, gz_out, xs_out, b_out, c_out, dt_out,
                  ext_scr, perm_scr, *, tm):
    s = pl.program_id(1)
    x = x_ref[...]
    ms = jnp.mean(x * x, axis=-1, keepdims=True)
    h = (x * lax.rsqrt(ms + EPS) * mixg_ref[...]).astype(jnp.bfloat16)

    def head_norm(t, gain):
        t2 = t * t
        hi = t2.astype(jnp.bfloat16)
        lo = (t2 - hi.astype(jnp.float32)).astype(jnp.bfloat16)
        ss = _dot(hi, hsum_ref[...]) + _dot(lo, hsum_ref[...])
        return t * lax.rsqrt(ss * (1.0 / HEAD_DIM) + EPS) * gain

    for a, out in enumerate((q_out, k_out, v_out)):
        t = _dot(h, wqkv_ref[:, a * ATTN_WIDTH:(a + 1) * ATTN_WIDTH])
        if a == 0:
            t = head_norm(t, qg_ref[...])
        elif a == 1:
            t = head_norm(t, kg_ref[...])
        for sl in range(N_SLABS):
            perm_scr[a * N_SLABS + sl] = t[:, sl * LANES:(sl + 1) * LANES]
        for sl in range(N_SLABS):
            for r in range(RES):
                rows = perm_scr[a * N_SLABS + sl, pl.ds(r, tm // RES, stride=RES), :]
                out[sl, r] = rows.astype(jnp.bfloat16)

    gz_out[...] = _silu(_dot(h, wz_ref[...])).astype(jnp.bfloat16)

    @pl.when(s == 0)
    def _():
        ext_scr[0:8, :] = jnp.zeros((8, CONV_DIM), jnp.float32)

    ext_scr[8:8 + tm, :] = _dot(h, wxbc_ref[...])
    acc = convb_ref[...] + convw_ref[CONV_WIDTH - 1:CONV_WIDTH, :] * ext_scr[8:8 + tm, :]
    for j in range(1, CONV_WIDTH):
        acc = acc + convw_ref[CONV_WIDTH - 1 - j:CONV_WIDTH - j, :] * ext_scr[8 - j:8 - j + tm, :]
    act = _silu(acc).astype(jnp.bfloat16)
    xs_out[...] = act[:, :SSM_WIDTH]
    b_out[...] = act[:, SSM_WIDTH:SSM_WIDTH + BC_WIDTH]
    c_out[...] = act[:, SSM_WIDTH + BC_WIDTH:]
    ext_scr[0:8, :] = ext_scr[tm:tm + 8, :]

    dt_out[...] = jax.nn.softplus(_dot(h, wdt_ref[...]) + dtb_ref[...])


def _in_proj(x, wqkv, wz, wxbc, wdt, mixg, qg, kg, hsum, convw, convb, dtb, *, tm):
    b, s, d = x.shape
    ns = s // SPAN
    tps = SPAN // tm
    grid = (b, s // tm)
    const = lambda shape: pl.BlockSpec(shape, lambda i, j: (0,) * len(shape), pipeline_mode=pl.Buffered(1))
    tok = lambda w: pl.BlockSpec((None, tm, w), lambda i, j: (i, j, 0))
    qkv_spec = pl.BlockSpec((None, None, N_SLABS, RES, tm // RES, LANES),
                            lambda i, j: (i, j // tps, 0, 0, j % tps, 0))
    qkv_shape = jax.ShapeDtypeStruct((b, ns, N_SLABS, RES, ROWS, LANES), jnp.bfloat16)
    return pl.pallas_call(
        functools.partial(_in_proj_body, tm=tm),
        grid=grid,
        in_specs=[tok(d), const(wqkv.shape), const(wz.shape), const(wxbc.shape), const(wdt.shape),
                  const(mixg.shape), const(qg.shape), const(kg.shape), const(hsum.shape),
                  const(convw.shape), const(convb.shape), const(dtb.shape)],
        out_specs=[qkv_spec, qkv_spec, qkv_spec, tok(SSM_WIDTH), tok(SSM_WIDTH), tok(BC_WIDTH), tok(BC_WIDTH),
                   tok(LANES)],
        out_shape=[qkv_shape, qkv_shape, qkv_shape,
                   jax.ShapeDtypeStruct((b, s, SSM_WIDTH), jnp.bfloat16),
                   jax.ShapeDtypeStruct((b, s, SSM_WIDTH), jnp.bfloat16),
                   jax.ShapeDtypeStruct((b, s, BC_WIDTH), jnp.bfloat16),
                   jax.ShapeDtypeStruct((b, s, BC_WIDTH), jnp.bfloat16),
                   jax.ShapeDtypeStruct((b, s, LANES), jnp.float32)],
        scratch_shapes=[pltpu.VMEM((tm + 8, CONV_DIM), jnp.float32),
                        pltpu.VMEM((3 * N_SLABS, tm, LANES), jnp.float32)],
        compiler_params=pltpu.CompilerParams(dimension_semantics=("arbitrary", "arbitrary"),
                                             vmem_limit_bytes=VMEM_LIMIT),
        name="in_proj",
    )(x, wqkv, wz, wxbc, wdt, mixg, qg, kg, hsum, convw, convb, dtb)


def _attention_masks():
    rho = np.arange(WIN)
    col = np.arange(2 * WIN)
    uq0 = 16 * (rho % 8) + rho // 8
    half0 = (col % 16) // 8
    uk0 = 16 * (col % 8) + col // 16 + (half0 - 1) * WIN
    uq1 = 4 * (rho % 32) + rho // 32
    half1 = (col % 64) // 32
    uk1 = 4 * (col % 32) + col // 64 + (half1 - 1) * WIN
    uq2 = rho
    half2 = col // WIN
    uk2 = col % WIN + (half2 - 1) * WIN
    out = np.zeros((3, 2, WIN, 2 * WIN), np.float32)
    for p, (uq, uk, half) in enumerate(((uq0, uk0, half0), (uq1, uk1, half1), (uq2, uk2, half2))):
        dist = uq[:, None] - uk[None, :]
        vis = (dist >= 0) & (dist <= WIN)
        out[p, 0] = np.where(vis, 0.0, NEG)
        out[p, 1] = np.where(vis & (half[None, :] == 1), 0.0, NEG)
    return out.reshape(6, WIN, 2 * WIN)


def _attn_unit(q, k, v, mask):
    lo_q = lax.broadcasted_iota(jnp.int32, (1, LANES), 1) < HEAD_DIM
    zero = jnp.zeros((), jnp.bfloat16)
    qq = jnp.concatenate([jnp.where(lo_q, q, zero), jnp.where(lo_q, zero, q)], axis=0)
    sc = _dot_nt(qq, k)
    sa = sc[:WIN] + mask
    sb = sc[WIN:] + mask
    ma = jnp.max(sa, axis=-1, keepdims=True)
    mb = jnp.max(sb, axis=-1, keepdims=True)
    pa = jnp.exp2(sa - ma)
    pb = jnp.exp2(sb - mb)
    la = jnp.sum(pa, axis=-1, keepdims=True)
    lb = jnp.sum(pb, axis=-1, keepdims=True)
    va = jnp.where(lo_q, v, zero)
    vb = jnp.where(lo_q, zero, v)
    o = _dot(pa.astype(jnp.bfloat16), va) + _dot(pb.astype(jnp.bfloat16), vb)
    m = jnp.where(lo_q, ma, mb)
    l = jnp.where(lo_q, la, lb)
    return o, m, l


def _attn_merge(o, m, l, ao, am, al):
    mx = jnp.maximum(m, am)
    wa = jnp.exp2(am - mx)
    wb = jnp.exp2(m - mx)
    return ao * wa + o * wb, mx, al * wa + l * wb


def _attention_body(q_ref, k_ref, kp_ref, v_ref, vp_ref, mask_ref, o_ref,
                    kc, vc, qf, kcf, vcf, acc_o, acc_m, acc_l, nat):
    first_span = pl.program_id(1) == 0
    tail = ROWS // 4
    kc[:, 0:tail, :] = kp_ref[:, ROWS - tail:, :]
    vc[:, 0:tail, :] = vp_ref[:, ROWS - tail:, :]
    kc[:, tail:, :] = k_ref[...]
    vc[:, tail:, :] = v_ref[...]
    qf[...] = q_ref[...].astype(jnp.float32)
    kcf[...] = kc[...].astype(jnp.float32)
    vcf[...] = vc[...].astype(jnp.float32)

    def p3(r, carry):
        q = q_ref[r]
        k = jnp.concatenate([kp_ref[r], k_ref[r]], axis=0)
        v = jnp.concatenate([vp_ref[r], v_ref[r]], axis=0)
        o, m, l = _attn_unit(q, k, v, mask_ref[4 + first_span.astype(jnp.int32)])
        acc_o[r] = o
        acc_m[r] = m
        acc_l[r] = l
        return carry

    for r in range(RES):
        p3(r, 0)

    def p2(t, carry):
        b2 = t // 4
        r4 = t % 4
        off = b2 * tail
        q = jnp.concatenate([q_ref[r4 + 4 * j, pl.ds(off, tail), :] for j in range(4)], axis=0)
        k = jnp.concatenate([kc[r4 + 4 * j, pl.ds(off, 2 * tail), :] for j in range(4)], axis=0)
        v = jnp.concatenate([vc[r4 + 4 * j, pl.ds(off, 2 * tail), :] for j in range(4)], axis=0)
        first = jnp.logical_and(first_span, b2 == 0).astype(jnp.int32)
        o, m, l = _attn_unit(q, k, v, mask_ref[2 + first])
        for j in range(4):
            rows = slice(j * tail, (j + 1) * tail)
            no, nm, nl = _attn_merge(o[rows], m[rows], l[rows],
                                     acc_o[r4 + 4 * j, pl.ds(off, tail), :],
                                     acc_m[r4 + 4 * j, pl.ds(off, tail), :],
                                     acc_l[r4 + 4 * j, pl.ds(off, tail), :])
            acc_o[r4 + 4 * j, pl.ds(off, tail), :] = no
            acc_m[r4 + 4 * j, pl.ds(off, tail), :] = nm
            acc_l[r4 + 4 * j, pl.ds(off, tail), :] = nl
        return carry

    for t in range(16):
        p2(t, 0)

    def p1(b, carry):
        off = b * 8
        koff = b * 8 + (tail - 8)
        q = jnp.concatenate([qf[r, pl.ds(off, 8), :] for r in range(RES)], axis=0).astype(jnp.bfloat16)
        k = jnp.concatenate([kcf[r, pl.ds(koff, 16), :] for r in range(RES)], axis=0).astype(jnp.bfloat16)
        v = jnp.concatenate([vcf[r, pl.ds(koff, 16), :] for r in range(RES)], axis=0).astype(jnp.bfloat16)
        first = jnp.logical_and(first_span, b == 0).astype(jnp.int32)
        o, m, l = _attn_unit(q, k, v, mask_ref[first])
        for r in range(RES):
            rows = slice(r * 8, (r + 1) * 8)
            no, _, nl = _attn_merge(o[rows], m[rows], l[rows],
                                    acc_o[r, pl.ds(off, 8), :], acc_m[r, pl.ds(off, 8), :],
                                    acc_l[r, pl.ds(off, 8), :])
            acc_o[r, pl.ds(off, 8), :] = no / nl
        return carry

    for b in range(SPAN // WIN):
        p1(b, 0)

    for r in range(RES):
        nat[pl.ds(r, ROWS, stride=RES), :] = acc_o[r]
    o_ref[...] = nat[...].astype(jnp.bfloat16)


def _attention(qr, kr, vr, masks):
    b, ns = qr.shape[0], qr.shape[1]
    cur = pl.BlockSpec((None, None, None, RES, ROWS, LANES), lambda i, j, c: (i, j, c, 0, 0, 0))
    prev = pl.BlockSpec((None, None, None, RES, ROWS, LANES), lambda i, j, c: (i, jnp.maximum(j - 1, 0), c, 0, 0, 0))
    f32 = jnp.float32
    stage_rows = ROWS + ROWS // 4
    return pl.pallas_call(
        _attention_body,
        grid=(b, ns, N_SLABS),
        in_specs=[cur, cur, prev, cur, prev, pl.BlockSpec(masks.shape, lambda i, j, c: (0, 0, 0))],
        out_specs=pl.BlockSpec((None, SPAN, LANES), lambda i, j, c: (i, j, c)),
        out_shape=jax.ShapeDtypeStruct((b, ns * SPAN, ATTN_WIDTH), jnp.bfloat16),
        scratch_shapes=[pltpu.VMEM((RES, stage_rows, LANES), jnp.bfloat16),
                        pltpu.VMEM((RES, stage_rows, LANES), jnp.bfloat16),
                        pltpu.VMEM((RES, ROWS, LANES), f32),
                        pltpu.VMEM((RES, stage_rows, LANES), f32),
                        pltpu.VMEM((RES, stage_rows, LANES), f32),
                        pltpu.VMEM((RES, ROWS, LANES), f32),
                        pltpu.VMEM((RES, ROWS, LANES), f32),
                        pltpu.VMEM((RES, ROWS, LANES), f32),
                        pltpu.VMEM((SPAN, LANES), f32)],
        compiler_params=pltpu.CompilerParams(dimension_semantics=("arbitrary", "arbitrary", "arbitrary"),
                                             vmem_limit_bytes=VMEM_LIMIT),
        name="attention",
    )(qr, kr, kr, vr, vr, masks)


def _ssd_body(xs_ref, b_ref, c_ref, dt_ref, gz_ref, a_ref, dskip_ref, gain_ref, ltri_ref, expand_ref,
              y_ref, state, *, n_chunks):
    @pl.when(pl.program_id(1) == 0)
    def _():
        state[...] = jnp.zeros_like(state)

    lane_lo = lax.broadcasted_iota(jnp.int32, (1, LANES), 1) < HEAD_DIM
    causal = (lax.broadcasted_iota(jnp.int32, (CHUNK, CHUNK), 0)
              >= lax.broadcasted_iota(jnp.int32, (CHUNK, CHUNK), 1))
    zero = jnp.zeros((), jnp.bfloat16)

    for c in range(n_chunks):
        rows = slice(c * CHUNK, (c + 1) * CHUNK)
        xs = xs_ref[rows, :]
        bm = b_ref[rows, :]
        cm = c_ref[rows, :]
        dt = dt_ref[rows, :]
        da = dt * a_ref[...]
        d_hi, d_mid, d_lo = _split3(da)
        ltri = ltri_ref[...]
        acs = _dot(ltri, d_hi) + _dot(ltri, d_mid) + _dot(ltri, d_lo)
        total = acs[CHUNK - 1:CHUNK, :]
        shift_t = (acs - jnp.log(dt)).T

        w_end = dt * jnp.exp(total - acs)
        w_hi = w_end.astype(jnp.bfloat16)
        e_acs = jnp.exp(acs)
        e_hi = e_acs.astype(jnp.bfloat16)
        w_exp = _dot(w_hi, expand_ref[...])
        e_exp = _dot(e_hi, expand_ref[...])
        dec = jnp.exp(total)
        dec8 = jnp.broadcast_to(dec, (8, LANES))
        dc_hi = dec8.astype(jnp.bfloat16)
        dc_lo = (dec8 - dc_hi.astype(jnp.float32)).astype(jnp.bfloat16)
        dec_exp = (_dot(dc_hi, expand_ref[...]) + _dot(dc_lo, expand_ref[...]))[0:1, :]

        xw = (xs.astype(jnp.float32) * w_exp).astype(jnp.bfloat16)

        for g in range(SSM_GROUPS):
            bg = bm[:, g * SSM_STATE:(g + 1) * SSM_STATE]
            cg = cm[:, g * SSM_STATE:(g + 1) * SSM_STATE]
            gmat = _dot_nt(cg, bg)
            glanes = slice(g * GROUP_WIDTH, (g + 1) * GROUP_WIDTH)
            st = state[g]
            y_off = _dot(cg, st.astype(jnp.bfloat16)) * e_exp[:, glanes]
            state[g] = st * dec_exp[:, glanes] + _dot_tn(bg, xw[:, glanes])
            y_parts = []
            for pair in range(HEADS_PER_GROUP // 2):
                h0 = g * HEADS_PER_GROUP + 2 * pair
                ms = []
                for hh in (h0, h0 + 1):
                    e = acs[:, hh:hh + 1] - shift_t[hh:hh + 1, :]
                    e = jnp.where(causal, e, NEG)
                    ms.append((gmat * jnp.exp(e)).astype(jnp.bfloat16))
                slab = xs[:, h0 * HEAD_DIM:(h0 + 2) * HEAD_DIM]
                rhs = jnp.concatenate([jnp.where(lane_lo, slab, zero), jnp.where(lane_lo, zero, slab)], axis=0)
                y_parts.append(_dot(jnp.concatenate(ms, axis=1), rhs))
            y = jnp.concatenate(y_parts, axis=1) + y_off
            xg = xs[:, glanes].astype(jnp.float32)
            y = y + dskip_ref[:, glanes] * xg
            u = y * gz_ref[rows, glanes].astype(jnp.float32)
            u = u * lax.rsqrt(jnp.mean(u * u, axis=-1, keepdims=True) + EPS)
            y_ref[rows, glanes] = (u * gain_ref[:, glanes]).astype(jnp.bfloat16)


def _ssd(xs, bm, cm, dt, gz, a_row, dskip, gain, ltri, expand, *, n_chunks):
    b, s, _ = xs.shape
    tt = n_chunks * CHUNK
    tok = lambda w: pl.BlockSpec((None, tt, w), lambda i, j: (i, j, 0))
    const = lambda arr: pl.BlockSpec(arr.shape, lambda i, j: (0,) * arr.ndim)
    return pl.pallas_call(
        functools.partial(_ssd_body, n_chunks=n_chunks),
        grid=(b, s // tt),
        in_specs=[tok(SSM_WIDTH), tok(BC_WIDTH), tok(BC_WIDTH), tok(LANES), tok(SSM_WIDTH),
                  const(a_row), const(dskip), const(gain), const(ltri), const(expand)],
        out_specs=tok(SSM_WIDTH),
        out_shape=jax.ShapeDtypeStruct((b, s, SSM_WIDTH), jnp.bfloat16),
        scratch_shapes=[pltpu.VMEM((SSM_GROUPS, SSM_STATE, GROUP_WIDTH), jnp.float32)],
        compiler_params=pltpu.CompilerParams(dimension_semantics=("arbitrary", "arbitrary"),
                                             vmem_limit_bytes=VMEM_LIMIT),
        name="ssd",
    )(xs, bm, cm, dt, gz, a_row, dskip, gain, ltri, expand)


def _rms(v, gain):
    return v * lax.rsqrt(jnp.mean(v * v, axis=-1, keepdims=True) + EPS) * gain


def _out_ffn_body(x_ref, attn_ref, y_ref, p_ref, wo_a_ref, wo_y_ref, fg_ref, wg_ref, wu_ref, wd_ref,
                  pg_ref, wpg_ref, bpg_ref, wple_ref, pn_ref, o_ref):
    x = x_ref[...] + _dot(attn_ref[...], wo_a_ref[...]) + _dot(y_ref[...], wo_y_ref[...])
    h = _rms(x, fg_ref[...]).astype(jnp.bfloat16)
    act = (_silu(_dot(h, wg_ref[...])) * _dot(h, wu_ref[...])).astype(jnp.bfloat16)
    x = x + _dot(act, wd_ref[...])
    hg = _rms(x, pg_ref[...]).astype(jnp.bfloat16)
    z = _dot(hg, wpg_ref[...]) + bpg_ref[...]
    gate = 1.0 / (1.0 + jnp.exp(-z))
    e = _rms(_dot(p_ref[...].astype(jnp.bfloat16), wple_ref[...]), pn_ref[...])
    o_ref[...] = x + gate * e


def _out_ffn(x, attn, y, p, wo_a, wo_y, fg, wg, wu, wd, pg, wpg, bpg, wple, pn, *, tm):
    b, s, d = x.shape
    tok = lambda w: pl.BlockSpec((None, tm, w), lambda i, j: (i, j, 0))
    const = lambda arr: pl.BlockSpec(arr.shape, lambda i, j: (0,) * arr.ndim, pipeline_mode=pl.Buffered(1))
    return pl.pallas_call(
        _out_ffn_body,
        grid=(b, s // tm),
        in_specs=[tok(d), tok(ATTN_WIDTH), tok(SSM_WIDTH), tok(p.shape[-1]),
                  const(wo_a), const(wo_y), const(fg), const(wg), const(wu), const(wd),
                  const(pg), const(wpg), const(bpg), const(wple), const(pn)],
        out_specs=tok(d),
        out_shape=jax.ShapeDtypeStruct((b, s, d), jnp.float32),
        compiler_params=pltpu.CompilerParams(dimension_semantics=("arbitrary", "arbitrary"),
                                             vmem_limit_bytes=VMEM_LIMIT),
        name="out_ffn",
    )(x, attn, y, p, wo_a, wo_y, fg, wg, wu, wd, pg, wpg, bpg, wple, pn)


def _layer(x, p, mix_norm, w_in, q_norm, k_norm, conv_w, conv_b, dt_bias, a_log, d_skip, ssm_norm, w_out,
           ffn_norm, w_ffn_gate, w_ffn_up, w_ffn_down, ple_gate_norm, w_ple_gate, b_ple_gate, w_ple, ple_norm):
    bf16 = jnp.bfloat16
    f32 = jnp.float32
    d = x.shape[-1]
    row = lambda v: v.reshape(1, -1).astype(f32)
    o_z = 3 * ATTN_WIDTH
    o_xbc = o_z + SSM_WIDTH
    o_dt = o_xbc + CONV_DIM
    wqkv = w_in[:, :o_z].astype(bf16)
    wz = w_in[:, o_z:o_xbc].astype(bf16)
    wxbc = w_in[:, o_xbc:o_dt].astype(bf16)
    wdt = jnp.pad(w_in[:, o_dt:], ((0, 0), (0, LANES - N_SSM_HEADS))).astype(bf16)
    dtb = jnp.pad(dt_bias.astype(f32), (0, LANES - N_SSM_HEADS)).reshape(1, LANES)
    qg = row(jnp.tile(q_norm, N_ATTN_HEADS)) * (HEAD_DIM ** -0.5 * LOG2E)
    kg = row(jnp.tile(k_norm, N_ATTN_HEADS))
    head_of = np.arange(ATTN_WIDTH) // HEAD_DIM
    hsum = jnp.asarray(head_of[:, None] == head_of[None, :], bf16)

    qr, kr, vr, gz, xs, bm, cm, dt = _in_proj(x, wqkv, wz, wxbc, wdt, row(mix_norm), qg, kg, hsum,
                                              conv_w.astype(f32), row(conv_b), dtb, tm=512)

    attn = _attention(qr, kr, vr, jnp.asarray(_attention_masks()))

    a_row = jnp.pad(-jnp.exp(a_log.astype(f32)), (0, LANES - N_SSM_HEADS)).reshape(1, LANES)
    tri = np.tril(np.ones((CHUNK, CHUNK), np.float32))
    ssm_head_of = np.arange(SSM_WIDTH) // HEAD_DIM
    expand = jnp.asarray(np.arange(LANES)[:, None] == ssm_head_of[None, :], bf16)
    y = _ssd(xs, bm, cm, dt, gz, a_row, row(jnp.repeat(d_skip, HEAD_DIM)), row(ssm_norm),
             jnp.asarray(tri, bf16), expand, n_chunks=4)

    return _out_ffn(x, attn, y, p,
                    w_out[:ATTN_WIDTH].astype(bf16), w_out[ATTN_WIDTH:].astype(bf16), row(ffn_norm),
                    w_ffn_gate.astype(bf16), w_ffn_up.astype(bf16), w_ffn_down.astype(bf16),
                    row(ple_gate_norm), w_ple_gate.astype(bf16), row(b_ple_gate), w_ple.astype(bf16),
                    row(ple_norm), tm=512)


def kernel(x, p, mix_norm, w_in, q_norm, k_norm, conv_w, conv_b, dt_bias, a_log, d_skip, ssm_norm, w_out, ffn_norm,
           w_ffn_gate, w_ffn_up, w_ffn_down, ple_gate_norm, w_ple_gate, b_ple_gate, w_ple, ple_norm):
    for i in range(p.shape[0]):
        x = _layer(x, p[i], mix_norm[i], w_in[i], q_norm[i], k_norm[i], conv_w[i], conv_b[i], dt_bias[i], a_log[i],
                   d_skip[i], ssm_norm[i], w_out[i], ffn_norm[i], w_ffn_gate[i], w_ffn_up[i], w_ffn_down[i],
                   ple_gate_norm[i], w_ple_gate[i], b_ple_gate[i], w_ple[i], ple_norm[i])
    return x
```

```python
import functools
import math

import jax
import jax.numpy as jnp
import numpy as np
from jax import lax
from jax.experimental import pallas as pl
from jax.experimental.pallas import tpu as pltpu

HEAD_DIM = 64
N_ATTN_HEADS = 8
ATTN_WIDTH = N_ATTN_HEADS * HEAD_DIM
N_SSM_HEADS = 24
SSM_WIDTH = N_SSM_HEADS * HEAD_DIM
SSM_GROUPS = 2
SSM_STATE = 128
GROUP_WIDTH = SSM_WIDTH // SSM_GROUPS
HEADS_PER_GROUP = N_SSM_HEADS // SSM_GROUPS
CONV_WIDTH = 4
BC_WIDTH = SSM_GROUPS * SSM_STATE
CONV_DIM = SSM_WIDTH + 2 * BC_WIDTH
EPS = 1e-6

LANES = 128
SPAN = 2048
RES = 16
ROWS = SPAN // RES
WIN = 128
N_SLABS = ATTN_WIDTH // LANES
CHUNK = 128
NEG = -1e30
IN_PROJ_TM = 512
SSD_FFN_TM = 512
FF_SPLIT = 768
ATTN_UNROLL = 16
LOG2E = math.log2(math.e)
VMEM_LIMIT = 60 * 1024 * 1024


def _sigmoid(v):
    return 0.5 * jnp.tanh(0.5 * v) + 0.5


def _silu(v):
    hv = 0.5 * v
    return hv * jnp.tanh(hv) + hv


def _split3(v):
    hi = v.astype(jnp.bfloat16)
    r1 = v - hi.astype(jnp.float32)
    mid = r1.astype(jnp.bfloat16)
    lo = (r1 - mid.astype(jnp.float32)).astype(jnp.bfloat16)
    return hi, mid, lo


def _dot(a, b):
    return jnp.dot(a, b, preferred_element_type=jnp.float32)


def _dot_nt(a, b):
    return lax.dot_general(a, b, (((1,), (1,)), ((), ())), preferred_element_type=jnp.float32)


def _dot_tn(a, b):
    return lax.dot_general(a, b, (((0,), (0,)), ((), ())), preferred_element_type=jnp.float32)


def _in_proj_body(x_ref, wqkv_ref, wz_ref, wxbc_ref, wdt_ref, mixg_ref, qg_ref, kg_ref, hsum_ref, dtb_ref,
                  q_out, k_out, v_out, gz_out, u_out, dt_out, perm_scr, *, tm):
    x = x_ref[...]
    ms = jnp.mean(x * x, axis=-1, keepdims=True)
    h = (x * lax.rsqrt(ms + EPS) * mixg_ref[...]).astype(jnp.bfloat16)

    def head_norm(t, gain):
        t2 = (t * t).astype(jnp.bfloat16)
        half = hsum_ref.shape[0]
        ss = jnp.concatenate([_dot(t2[:, i:i + half], hsum_ref[...]) for i in range(0, ATTN_WIDTH, half)], axis=1)
        return t * lax.rsqrt(ss * (1.0 / HEAD_DIM) + EPS) * gain

    for a, out in enumerate((q_out, k_out, v_out)):
        t = _dot(h, wqkv_ref[:, a * ATTN_WIDTH:(a + 1) * ATTN_WIDTH])
        if a == 0:
            t = head_norm(t, qg_ref[...])
        elif a == 1:
            t = head_norm(t, kg_ref[...])
        for sl in range(N_SLABS):
            perm_scr[a * N_SLABS + sl] = t[:, sl * LANES:(sl + 1) * LANES]
        for sl in range(N_SLABS):
            for r in range(RES):
                rows = perm_scr[a * N_SLABS + sl, pl.ds(r, tm // RES, stride=RES), :]
                out[sl, r] = rows.astype(jnp.bfloat16)

    gz_out[...] = _silu(_dot(h, wz_ref[...])).astype(jnp.bfloat16)
    u_out[...] = _dot(h, wxbc_ref[...]).astype(jnp.bfloat16)
    dt_out[...] = jax.nn.softplus(_dot(h, wdt_ref[...]) + dtb_ref[...])


def _in_proj(x, wqkv, wz, wxbc, wdt, mixg, qg, kg, hsum, dtb, *, tm):
    b, s, d = x.shape
    ns = s // SPAN
    tps = SPAN // tm
    grid = (b, s // tm)
    const = lambda shape: pl.BlockSpec(shape, lambda i, j: (0,) * len(shape), pipeline_mode=pl.Buffered(1))
    tok = lambda w: pl.BlockSpec((None, tm, w), lambda i, j: (i, j, 0))
    qkv_spec = pl.BlockSpec((None, None, N_SLABS, RES, tm // RES, LANES),
                            lambda i, j: (i, j // tps, 0, 0, j % tps, 0))
    qkv_shape = jax.ShapeDtypeStruct((b, ns, N_SLABS, RES, ROWS, LANES), jnp.bfloat16)
    return pl.pallas_call(
        functools.partial(_in_proj_body, tm=tm),
        grid=grid,
        in_specs=[tok(d), const(wqkv.shape), const(wz.shape), const(wxbc.shape), const(wdt.shape),
                  const(mixg.shape), const(qg.shape), const(kg.shape), const(hsum.shape), const(dtb.shape)],
        out_specs=[qkv_spec, qkv_spec, qkv_spec, tok(SSM_WIDTH), tok(CONV_DIM), tok(LANES)],
        out_shape=[qkv_shape, qkv_shape, qkv_shape,
                   jax.ShapeDtypeStruct((b, s, SSM_WIDTH), jnp.bfloat16),
                   jax.ShapeDtypeStruct((b, s, CONV_DIM), jnp.bfloat16),
                   jax.ShapeDtypeStruct((b, s, LANES), jnp.float32)],
        scratch_shapes=[pltpu.VMEM((3 * N_SLABS, tm, LANES), jnp.float32)],
        compiler_params=pltpu.CompilerParams(dimension_semantics=("arbitrary", "arbitrary"),
                                             vmem_limit_bytes=VMEM_LIMIT),
        name="in_proj",
    )(x, wqkv, wz, wxbc, wdt, mixg, qg, kg, hsum, dtb)


def _attention_masks():
    rho = np.arange(WIN)
    col = np.arange(2 * WIN)
    uq0 = 16 * (rho % 8) + rho // 8
    half0 = (col % 16) // 8
    uk0 = 16 * (col % 8) + col // 16 + (half0 - 1) * WIN
    uq1 = 4 * (rho % 32) + rho // 32
    half1 = (col % 64) // 32
    uk1 = 4 * (col % 32) + col // 64 + (half1 - 1) * WIN
    uq2 = rho
    half2 = col // WIN
    uk2 = col % WIN + (half2 - 1) * WIN
    out = np.zeros((3, 2, WIN, 2 * WIN), np.float32)
    for p, (uq, uk, half) in enumerate(((uq0, uk0, half0), (uq1, uk1, half1), (uq2, uk2, half2))):
        dist = uq[:, None] - uk[None, :]
        vis = (dist >= 0) & (dist <= WIN)
        out[p, 0] = np.where(vis, 0.0, NEG)
        out[p, 1] = np.where(vis & (half[None, :] == 1), 0.0, NEG)
    return out.reshape(6, WIN, 2 * WIN)


def _attn_unit(q, k, v, mask, ones_a, ones_b):
    lo_q = lax.broadcasted_iota(jnp.int32, (1, LANES), 1) < HEAD_DIM
    zero = jnp.zeros((), jnp.bfloat16)
    qq = jnp.concatenate([jnp.where(lo_q, q, zero), jnp.where(lo_q, zero, q)], axis=0)
    sc = _dot_nt(qq, k)
    sa = sc[:WIN] + mask
    sb = sc[WIN:] + mask
    ma = jnp.max(sa, axis=-1, keepdims=True)
    mb = jnp.max(sb, axis=-1, keepdims=True)
    pa = jnp.exp2(sa - ma).astype(jnp.bfloat16)
    pb = jnp.exp2(sb - mb).astype(jnp.bfloat16)
    va = jnp.concatenate([jnp.where(lo_q, v, zero), ones_a], axis=1)
    vb = jnp.concatenate([jnp.where(lo_q, zero, v), ones_b], axis=1)
    ol = _dot(jnp.concatenate([pa, pb], axis=1), jnp.concatenate([va, vb], axis=0))
    m = jnp.where(lo_q, ma, mb)
    return ol[:, :LANES], m, ol[:, LANES:]


def _attn_merge(o, m, l, ao, am, al):
    mx = jnp.maximum(m, am)
    wa = jnp.exp2(am - mx)
    wb = jnp.exp2(m - mx)
    return ao * wa + o * wb, mx, al * wa + l * wb


def _attention_body(q_ref, k_ref, kp_ref, v_ref, vp_ref, mask_ref, o_ref,
                    kc, vc, qf, kcf, vcf, acc_o, acc_m, acc_l, nat):
    first_span = pl.program_id(1) == 0
    lane_lo = lax.broadcasted_iota(jnp.int32, (2 * WIN, LANES), 1) < HEAD_DIM
    ones_a = jnp.where(lane_lo, 1.0, 0.0).astype(jnp.bfloat16)
    ones_b = jnp.where(lane_lo, 0.0, 1.0).astype(jnp.bfloat16)
    tail = ROWS // 4
    kc[:, 0:tail, :] = kp_ref[:, ROWS - tail:, :]
    vc[:, 0:tail, :] = vp_ref[:, ROWS - tail:, :]
    kc[:, tail:, :] = k_ref[...]
    vc[:, tail:, :] = v_ref[...]
    qf[...] = q_ref[...].astype(jnp.float32)
    kcf[...] = kc[...].astype(jnp.float32)
    vcf[...] = vc[...].astype(jnp.float32)

    def p3(r, carry):
        q = q_ref[r]
        k = jnp.concatenate([kp_ref[r], k_ref[r]], axis=0)
        v = jnp.concatenate([vp_ref[r], v_ref[r]], axis=0)
        o, m, l = _attn_unit(q, k, v, mask_ref[4 + first_span.astype(jnp.int32)], ones_a, ones_b)
        acc_o[r] = o
        acc_m[r] = m
        acc_l[r] = l
        return carry

    lax.fori_loop(0, RES, p3, 0, unroll=ATTN_UNROLL)

    def p2(t, carry):
        b2 = t // 4
        r4 = t % 4
        off = pl.multiple_of(b2 * tail, tail)
        q = jnp.concatenate([q_ref[r4 + 4 * j, pl.ds(off, tail), :] for j in range(4)], axis=0)
        k = jnp.concatenate([kc[r4 + 4 * j, pl.ds(off, 2 * tail), :] for j in range(4)], axis=0)
        v = jnp.concatenate([vc[r4 + 4 * j, pl.ds(off, 2 * tail), :] for j in range(4)], axis=0)
        first = jnp.logical_and(first_span, b2 == 0).astype(jnp.int32)
        o, m, l = _attn_unit(q, k, v, mask_ref[2 + first], ones_a, ones_b)
        for j in range(4):
            rows = slice(j * tail, (j + 1) * tail)
            no, nm, nl = _attn_merge(o[rows], m[rows], l[rows],
                                     acc_o[r4 + 4 * j, pl.ds(off, tail), :],
                                     acc_m[r4 + 4 * j, pl.ds(off, tail), :],
                                     acc_l[r4 + 4 * j, pl.ds(off, tail), :])
            acc_o[r4 + 4 * j, pl.ds(off, tail), :] = no
            acc_m[r4 + 4 * j, pl.ds(off, tail), :] = nm
            acc_l[r4 + 4 * j, pl.ds(off, tail), :] = nl
        return carry

    lax.fori_loop(0, 16, p2, 0, unroll=ATTN_UNROLL)

    def p1(b, carry):
        off = pl.multiple_of(b * 8, 8)
        koff = pl.multiple_of(b * 8 + (tail - 8), 8)
        q = jnp.concatenate([qf[r, pl.ds(off, 8), :] for r in range(RES)], axis=0).astype(jnp.bfloat16)
        k = jnp.concatenate([kcf[r, pl.ds(koff, 16), :] for r in range(RES)], axis=0).astype(jnp.bfloat16)
        v = jnp.concatenate([vcf[r, pl.ds(koff, 16), :] for r in range(RES)], axis=0).astype(jnp.bfloat16)
        first = jnp.logical_and(first_span, b == 0).astype(jnp.int32)
        o, m, l = _attn_unit(q, k, v, mask_ref[first], ones_a, ones_b)
        for r in range(RES):
            rows = slice(r * 8, (r + 1) * 8)
            no, _, nl = _attn_merge(o[rows], m[rows], l[rows],
                                    acc_o[r, pl.ds(off, 8), :], acc_m[r, pl.ds(off, 8), :],
                                    acc_l[r, pl.ds(off, 8), :])
            acc_o[r, pl.ds(off, 8), :] = no / nl
        return carry

    lax.fori_loop(0, SPAN // WIN, p1, 0, unroll=ATTN_UNROLL)

    for r in range(RES):
        nat[pl.ds(r, ROWS, stride=RES), :] = acc_o[r]
    o_ref[...] = nat[...].astype(jnp.bfloat16)


def _attention(qr, kr, vr, masks):
    b, ns = qr.shape[0], qr.shape[1]
    cur = pl.BlockSpec((None, None, None, RES, ROWS, LANES), lambda i, j, c: (i, j, c, 0, 0, 0))
    prev = pl.BlockSpec((None, None, None, RES, ROWS, LANES), lambda i, j, c: (i, jnp.maximum(j - 1, 0), c, 0, 0, 0))
    f32 = jnp.float32
    stage_rows = ROWS + ROWS // 4
    return pl.pallas_call(
        _attention_body,
        grid=(b, ns, N_SLABS),
        in_specs=[cur, cur, prev, cur, prev, pl.BlockSpec(masks.shape, lambda i, j, c: (0, 0, 0))],
        out_specs=pl.BlockSpec((None, SPAN, LANES), lambda i, j, c: (i, j, c)),
        out_shape=jax.ShapeDtypeStruct((b, ns * SPAN, ATTN_WIDTH), jnp.bfloat16),
        scratch_shapes=[pltpu.VMEM((RES, stage_rows, LANES), jnp.bfloat16),
                        pltpu.VMEM((RES, stage_rows, LANES), jnp.bfloat16),
                        pltpu.VMEM((RES, ROWS, LANES), f32),
                        pltpu.VMEM((RES, stage_rows, LANES), f32),
                        pltpu.VMEM((RES, stage_rows, LANES), f32),
                        pltpu.VMEM((RES, ROWS, LANES), f32),
                        pltpu.VMEM((RES, ROWS, LANES), f32),
                        pltpu.VMEM((RES, ROWS, LANES), f32),
                        pltpu.VMEM((SPAN, LANES), f32)],
        compiler_params=pltpu.CompilerParams(dimension_semantics=("arbitrary", "arbitrary", "arbitrary"),
                                             vmem_limit_bytes=VMEM_LIMIT),
        name="attention",
    )(qr, kr, kr, vr, vr, masks)


def _dependent_zero(v):
    bits = pltpu.bitcast(v, jnp.int32)
    z = lax.shift_right_logical(lax.shift_right_logical(bits, 16), 16).astype(jnp.float32)[0:1, :]
    return jnp.concatenate([z] * 8, axis=1)


def _rms(v, gain):
    return v * lax.rsqrt(jnp.mean(v * v, axis=-1, keepdims=True) + EPS) * gain


def _ssd_chunk(c, tail, dep, u_ref, dt_ref, gz_ref, convw_ref, convb_ref, a_ref, dskip_ref, gain_ref, ltri_ref,
               expand_ref, y_dst, state):
    lane_lo = lax.broadcasted_iota(jnp.int32, (1, LANES), 1) < HEAD_DIM
    causal = (lax.broadcasted_iota(jnp.int32, (CHUNK, CHUNK), 0)
              >= lax.broadcasted_iota(jnp.int32, (CHUNK, CHUNK), 1))
    zero = jnp.zeros((), jnp.bfloat16)
    rows = slice(c * CHUNK, (c + 1) * CHUNK)
    u = u_ref[rows, :].astype(jnp.float32)
    ucat = jnp.concatenate([tail, u], axis=0)
    bias = convb_ref[...] if dep is None else convb_ref[...] + jnp.concatenate([dep, dep], axis=1)
    acc = bias + convw_ref[CONV_WIDTH - 1:CONV_WIDTH, :] * u
    for j in range(1, CONV_WIDTH):
        acc = acc + convw_ref[CONV_WIDTH - 1 - j:CONV_WIDTH - j, :] * ucat[8 - j:8 - j + CHUNK, :]
    act = _silu(acc).astype(jnp.bfloat16)
    xs = act[:, :SSM_WIDTH]
    bm = act[:, SSM_WIDTH:SSM_WIDTH + BC_WIDTH]
    cm = act[:, SSM_WIDTH + BC_WIDTH:]

    dt = dt_ref[rows, :]
    da = dt * a_ref[...]
    d_hi, d_mid, d_lo = _split3(da)
    ltri = ltri_ref[...]
    acs = _dot(ltri, d_hi) + _dot(ltri, d_mid) + _dot(ltri, d_lo)
    total = acs[CHUNK - 1:CHUNK, :]
    shift_t = (acs - jnp.log2(dt)).T

    w_end = (dt * jnp.exp2(total - acs)).astype(jnp.bfloat16)
    e_acs = jnp.exp2(acs).astype(jnp.bfloat16)
    w_exp = _dot(w_end, expand_ref[...])
    e_exp = _dot(e_acs, expand_ref[...])
    dec8 = jnp.broadcast_to(jnp.exp2(total), (8, LANES))
    dc_hi = dec8.astype(jnp.bfloat16)
    dc_lo = (dec8 - dc_hi.astype(jnp.float32)).astype(jnp.bfloat16)
    dec_exp = (_dot(dc_hi, expand_ref[...]) + _dot(dc_lo, expand_ref[...]))[0:1, :]

    xw = (xs.astype(jnp.float32) * w_exp).astype(jnp.bfloat16)
    zs = []
    for g in range(SSM_GROUPS):
        bg = bm[:, g * SSM_STATE:(g + 1) * SSM_STATE]
        cg = cm[:, g * SSM_STATE:(g + 1) * SSM_STATE]
        gmat = _dot_nt(cg, bg)
        glanes = slice(g * GROUP_WIDTH, (g + 1) * GROUP_WIDTH)
        st = state[g]
        y_off = _dot(cg, st.astype(jnp.bfloat16)) * e_exp[:, glanes]
        state[g] = st * dec_exp[:, glanes] + _dot_tn(bg, xw[:, glanes])
        y_parts = []
        for pair in range(HEADS_PER_GROUP // 2):
            h0 = g * HEADS_PER_GROUP + 2 * pair
            ms = []
            for hh in (h0, h0 + 1):
                e = acs[:, hh:hh + 1] - shift_t[hh:hh + 1, :]
                e = jnp.where(causal, e, NEG)
                ms.append((gmat * jnp.exp2(e)).astype(jnp.bfloat16))
            slab = xs[:, h0 * HEAD_DIM:(h0 + 2) * HEAD_DIM]
            rhs = jnp.concatenate([jnp.where(lane_lo, slab, zero), jnp.where(lane_lo, zero, slab)], axis=0)
            y_parts.append(_dot(jnp.concatenate(ms, axis=1), rhs))
        y = jnp.concatenate(y_parts, axis=1) + y_off
        y = y + dskip_ref[:, glanes] * xs[:, glanes].astype(jnp.float32)
        uu = y * gz_ref[rows, glanes].astype(jnp.float32)
        uu = uu * lax.rsqrt(jnp.mean(uu * uu, axis=-1, keepdims=True) + EPS)
        yo = uu * gain_ref[:, glanes]
        y_dst[rows, glanes] = yo.astype(jnp.bfloat16)
        zs.append(_dependent_zero(yo[CHUNK - 8:, 0:LANES]))
    return u[CHUNK - 8:, :], zs[0] + zs[1]


def _ssd_ffn_body(u_ref, dt_ref, gz_ref, x_ref, attn_ref, p_ref,
                  convw_ref, convb_ref, a_ref, dskip_ref, gain_ref, ltri_ref, expand_ref,
                  wo_a_ref, wo_y_ref, fg_ref, wg_ref, wu_ref, wd_ref, pg_ref, wpg_ref, bpg_ref, wple_ref, pn_ref,
                  o_ref, y_scr, state, tail_scr, *, n_chunks, tiles_per_seq, n_tiles):
    i = pl.program_id(0)
    j = jnp.minimum(i, n_tiles - 1)

    @pl.when(i == 0)
    def _():
        y_scr[...] = jnp.zeros_like(y_scr)

    @pl.when(j % tiles_per_seq == 0)
    def _():
        state[...] = jnp.zeros_like(state)
        tail_scr[...] = jnp.zeros_like(tail_scr)

    y_dst = y_scr.at[i % 2]
    y_src = y_scr.at[(i + 1) % 2]
    chunk = functools.partial(_ssd_chunk, u_ref=u_ref, dt_ref=dt_ref, gz_ref=gz_ref, convw_ref=convw_ref,
                              convb_ref=convb_ref, a_ref=a_ref, dskip_ref=dskip_ref, gain_ref=gain_ref,
                              ltri_ref=ltri_ref, expand_ref=expand_ref, y_dst=y_dst, state=state)
    d_ff = wg_ref.shape[1]
    bounds = list(range(0, d_ff, FF_SPLIT)) + [d_ff]
    pieces = list(zip(bounds[:-1], bounds[1:]))

    x = x_ref[...] + _dot(attn_ref[...], wo_a_ref[...]) + _dot(y_src[...], wo_y_ref[...])
    zp = [_dependent_zero(x[-8:, 0:LANES])]
    xn = x * lax.rsqrt(jnp.mean(x * x, axis=-1, keepdims=True) + EPS)
    tail = tail_scr[...]
    zc = []
    acts = []
    for c in range(n_chunks):
        tail, z = chunk(c, tail, zp[c - 1] if c >= 1 else None)
        zc.append(z)
        if c < len(pieces):
            lo, hi = pieces[c]
            h = (xn * (fg_ref[...] + (zc[c - 1] if c >= 1 else 0.0))).astype(jnp.bfloat16)
            act = (_silu(_dot(h, wg_ref[:, lo:hi])) * _dot(h, wu_ref[:, lo:hi])).astype(jnp.bfloat16)
            acts.append(act)
            zp.append(_dependent_zero(act[-8:, 0:LANES].astype(jnp.float32)))
    tail_scr[...] = tail
    for k in range(n_chunks, len(pieces)):
        lo, hi = pieces[k]
        h = (xn * (fg_ref[...] + zc[-1])).astype(jnp.bfloat16)
        acts.append((_silu(_dot(h, wg_ref[:, lo:hi])) * _dot(h, wu_ref[:, lo:hi])).astype(jnp.bfloat16))
    x = x + _dot(jnp.concatenate(acts, axis=1), wd_ref[...])
    hg = _rms(x, pg_ref[...]).astype(jnp.bfloat16)
    gate = _sigmoid(_dot(hg, wpg_ref[...]) + bpg_ref[...])
    e = _rms(_dot(p_ref[...].astype(jnp.bfloat16), wple_ref[...]), pn_ref[...])
    o_ref[...] = x + gate * e


def _ssd_ffn(u, dt, gz, x, attn, p, ssd_consts, ffn_consts, *, tm):
    b, s, d = x.shape
    tps = s // tm
    n_tiles = b * tps
    ssd_tile = lambda i: jnp.minimum(i, n_tiles - 1)
    ffn_tile = lambda i: jnp.maximum(i - 1, 0)
    tok = lambda w, tile: pl.BlockSpec((None, tm, w), lambda i: (tile(i) // tps, tile(i) % tps, 0))
    const = lambda arr: pl.BlockSpec(arr.shape, lambda i: (0,) * arr.ndim, pipeline_mode=pl.Buffered(1))
    return pl.pallas_call(
        functools.partial(_ssd_ffn_body, n_chunks=tm // CHUNK, tiles_per_seq=tps, n_tiles=n_tiles),
        grid=(n_tiles + 1,),
        in_specs=[tok(CONV_DIM, ssd_tile), tok(LANES, ssd_tile), tok(SSM_WIDTH, ssd_tile),
                  tok(d, ffn_tile), tok(ATTN_WIDTH, ffn_tile), tok(p.shape[-1], ffn_tile)]
                 + [const(c) for c in ssd_consts] + [const(c) for c in ffn_consts],
        out_specs=tok(d, ffn_tile),
        out_shape=jax.ShapeDtypeStruct((b, s, d), jnp.float32),
        scratch_shapes=[pltpu.VMEM((2, tm, SSM_WIDTH), jnp.bfloat16),
                        pltpu.VMEM((SSM_GROUPS, SSM_STATE, GROUP_WIDTH), jnp.float32),
                        pltpu.VMEM((8, CONV_DIM), jnp.float32)],
        compiler_params=pltpu.CompilerParams(dimension_semantics=("arbitrary",),
                                             vmem_limit_bytes=VMEM_LIMIT),
        name="ssd_ffn",
    )(u, dt, gz, x, attn, p, *ssd_consts, *ffn_consts)


def _layer(x, p, mix_norm, w_in, q_norm, k_norm, conv_w, conv_b, dt_bias, a_log, d_skip, ssm_norm, w_out,
           ffn_norm, w_ffn_gate, w_ffn_up, w_ffn_down, ple_gate_norm, w_ple_gate, b_ple_gate, w_ple, ple_norm):
    bf16 = jnp.bfloat16
    f32 = jnp.float32
    row = lambda v: v.reshape(1, -1).astype(f32)
    o_z = 3 * ATTN_WIDTH
    o_xbc = o_z + SSM_WIDTH
    o_dt = o_xbc + CONV_DIM
    wqkv = w_in[:, :o_z].astype(bf16)
    wz = w_in[:, o_z:o_xbc].astype(bf16)
    wxbc = w_in[:, o_xbc:o_dt].astype(bf16)
    wdt = jnp.pad(w_in[:, o_dt:], ((0, 0), (0, LANES - N_SSM_HEADS))).astype(bf16)
    dtb = jnp.pad(dt_bias.astype(f32), (0, LANES - N_SSM_HEADS)).reshape(1, LANES)
    qg = row(jnp.tile(q_norm, N_ATTN_HEADS)) * (HEAD_DIM ** -0.5 * LOG2E)
    kg = row(jnp.tile(k_norm, N_ATTN_HEADS))
    head_of = np.arange(2 * LANES) // HEAD_DIM
    hsum = jnp.asarray(head_of[:, None] == head_of[None, :], bf16)

    qr, kr, vr, gz, u, dt = _in_proj(x, wqkv, wz, wxbc, wdt, row(mix_norm), qg, kg, hsum, dtb, tm=IN_PROJ_TM)

    attn = _attention(qr, kr, vr, jnp.asarray(_attention_masks()))

    a_row = jnp.pad(-jnp.exp(a_log.astype(f32)) * LOG2E, (0, LANES - N_SSM_HEADS)).reshape(1, LANES)
    tri = np.tril(np.ones((CHUNK, CHUNK), np.float32))
    ssm_head_of = np.arange(SSM_WIDTH) // HEAD_DIM
    expand = jnp.asarray(np.arange(LANES)[:, None] == ssm_head_of[None, :], bf16)
    ssd_consts = (conv_w.astype(f32), row(conv_b), a_row, row(jnp.repeat(d_skip, HEAD_DIM)), row(ssm_norm),
                  jnp.asarray(tri, bf16), expand)
    ffn_consts = (w_out[:ATTN_WIDTH].astype(bf16), w_out[ATTN_WIDTH:].astype(bf16), row(ffn_norm),
                  w_ffn_gate.astype(bf16), w_ffn_up.astype(bf16), w_ffn_down.astype(bf16),
                  row(ple_gate_norm), w_ple_gate.astype(bf16), row(b_ple_gate), w_ple.astype(bf16), row(ple_norm))
    return _ssd_ffn(u, dt, gz, x, attn, p, ssd_consts, ffn_consts, tm=SSD_FFN_TM)


def kernel(x, p, mix_norm, w_in, q_norm, k_norm, conv_w, conv_b, dt_bias, a_log, d_skip, ssm_norm, w_out, ffn_norm,
           w_ffn_gate, w_ffn_up, w_ffn_down, ple_gate_norm, w_ple_gate, b_ple_gate, w_ple, ple_norm):
    for i in range(p.shape[0]):
        x = _layer(x, p[i], mix_norm[i], w_in[i], q_norm[i], k_norm[i], conv_w[i], conv_b[i], dt_bias[i], a_log[i],
                   d_skip[i], ssm_norm[i], w_out[i], ffn_norm[i], w_ffn_gate[i], w_ffn_up[i], w_ffn_down[i],
                   ple_gate_norm[i], w_ple_gate[i], b_ple_gate[i], w_ple[i], ple_norm[i])
    return x
```
